```python
import jax, jax.numpy as jnp
from jax import lax
import numpy as np

D_MODEL = 1024
BATCH = 4
SEQ = 4096
DEPTH = 1

CHUNK = 64
N_META = 16
D_A = 1024
D_B = 1024
CONV_A = 31
CONV_B = 3
EPS = 1e-6
SPLITS = (D_A, D_A, D_A, D_B, D_B, D_B, D_B, D_MODEL, D_MODEL)
D_IN = sum(SPLITS)

kernel_name = "hybrid_gated_conformer_shortconv_block"


def _rmsnorm(x, g):
    xf = x.astype(jnp.float32)
    y = xf * lax.rsqrt(jnp.mean(xf * xf, axis=-1, keepdims=True) + EPS)
    return (y * g.astype(jnp.float32)).astype(x.dtype)


def _layernorm(x, g, b):
    xf = x.astype(jnp.float32)
    mu = jnp.mean(xf, axis=-1, keepdims=True)
    var = jnp.mean(jnp.square(xf - mu), axis=-1, keepdims=True)
    y = (xf - mu) * lax.rsqrt(var + EPS)
    return (y * g.astype(jnp.float32) + b.astype(jnp.float32)).astype(x.dtype)


def _causal_dwconv(x, w):
    k = w.shape[0]
    return lax.conv_general_dilated(
        x, w.astype(x.dtype)[:, None, :], window_strides=(1,), padding=[(k - 1, 0)],
        dimension_numbers=("NWC", "WIO", "NWC"), feature_group_count=x.shape[-1])


def setup_inputs(seed: int = 0) -> dict:
    key = jax.random.key(seed)
    ks = jax.random.split(key, 16)
    f = jnp.float32
    L = DEPTH
    nrm = lambda k, shp, s: jax.random.normal(k, shp, f) * s
    return {
        "x": jax.random.normal(ks[0], (BATCH, SEQ, D_MODEL), f),
        "meta_tokens": nrm(ks[1], (N_META, D_MODEL), 1.0),
        "norm_g": 1.0 + nrm(ks[2], (L, D_MODEL), 0.02),
        "w_in": nrm(ks[3], (L, D_MODEL, D_IN), D_MODEL ** -0.5),
        "conv_a_w": nrm(ks[4], (L, CONV_A, D_A), CONV_A ** -0.5),
        "conv_a_b": nrm(ks[5], (L, D_A), 0.02),
        "ln_a_g": 1.0 + nrm(ks[6], (L, D_A), 0.02),
        "ln_a_b": nrm(ks[7], (L, D_A), 0.02),
        "w_a_out": nrm(ks[8], (L, D_A, D_MODEL), D_A ** -0.5),
        "b_a_out": nrm(ks[9], (L, D_MODEL), 0.02),
        "conv_b_w": nrm(ks[10], (L, CONV_B, D_B), CONV_B ** -0.5),
        "w_b_out": nrm(ks[11], (L, D_B, D_MODEL), D_B ** -0.5),
        "w_out": nrm(ks[12], (L, D_MODEL, D_MODEL), D_MODEL ** -0.5),
        "final_g": 1.0 + nrm(ks[13], (D_MODEL,), 0.02),
    }


def reference(x, meta_tokens, norm_g, w_in, conv_a_w, conv_a_b, ln_a_g, ln_a_b,
              w_a_out, b_a_out, conv_b_w, w_b_out, w_out, final_g):
    bsz = x.shape[0]
    meta = jnp.broadcast_to(meta_tokens.astype(x.dtype)[None], (bsz, N_META, D_MODEL))
    s = jnp.concatenate([meta, x], axis=1)
    idx = np.cumsum(SPLITS)[:-1].tolist()
    for l in range(DEPTH):
        h = _rmsnorm(s, norm_g[l])
        proj = jnp.einsum("bld,de->ble", h, w_in[l])
        a_val, a_glu, a_z, b_B, b_C, b_x, b_z, g_a, g_b = jnp.split(proj, idx, axis=-1)
        ua = a_val * jax.nn.sigmoid(a_glu)
        ua = _causal_dwconv(ua, conv_a_w[l]) + conv_a_b[l]
        ua = jax.nn.silu(_layernorm(ua, ln_a_g[l], ln_a_b[l]))
        ya = jnp.einsum("blc,cd->bld", ua * jax.nn.silu(a_z), w_a_out[l]) + b_a_out[l]
        ub = b_B * _causal_dwconv(b_C * b_x, conv_b_w[l])
        yb = jnp.einsum("blc,cd->bld", ub * jax.nn.silu(b_z), w_b_out[l])
        m = jax.nn.sigmoid(g_a) * ya + jax.nn.sigmoid(g_b) * yb
        s = s + jnp.einsum("bld,de->ble", m, w_out[l])
    y = _rmsnorm(s, final_g)
    return y[:, N_META:, :]
```

```python
import functools

import jax
import jax.numpy as jnp
from jax import lax
from jax.experimental import pallas as pl
from jax.experimental.pallas import tpu as pltpu

D = 1024
N_META = 16
CONV_A = 31
CONV_B = 3
EPS = 1e-6

SUBLANES = 8
LANES = 128
TM = 256
CB = 256
HALO_A = 32
HALO_B = 8
VMEM_LIMIT_BYTES = 56 * 1024 * 1024

O_AV, O_AG, O_AZ, O_BB, O_BC, O_BX, O_BZ, O_GA, O_GB = (i * D for i in range(9))


def _dot(a, b):
    return jnp.dot(a, b, preferred_element_type=jnp.float32)


def _sigmoid(v):
    return 1.0 / (1.0 + jnp.exp(-v))


def _silu(v):
    return v * _sigmoid(v)


def _rms_scale(v):
    return v * lax.rsqrt(jnp.mean(v * v, axis=-1, keepdims=True) + EPS)


def _causal_conv(buf, wb_ref, ktaps, halo_groups, lanes, n_groups, emit):
    nq = (ktaps - 1) // SUBLANES + 1
    nr = min(SUBLANES, ktaps)
    width = lanes.stop - lanes.start
    row = lax.broadcasted_iota(jnp.int32, (SUBLANES, width), 0)
    masks = [row >= r for r in range(nr)]
    prev = None
    for g in range(-1, n_groups):
        us = []
        for q in range(nq):
            r0 = SUBLANES * (halo_groups + g - q)
            us.append(buf[r0:r0 + SUBLANES, lanes])
        rolled = []
        for r in range(nr):
            if g < 0 and r == 0:
                rolled.append(None)
                continue
            acc = None
            for q in range(nq):
                s = SUBLANES * q + r
                if s >= ktaps:
                    continue
                term = wb_ref[ktaps - 1 - s, :, lanes] * us[q]
                acc = term if acc is None else acc + term
            rolled.append(acc if r == 0 else pltpu.roll(acc, r, 0))
        if g >= 0:
            out = rolled[0]
            for r in range(1, nr):
                out = out + jnp.where(masks[r], rolled[r], prev[r])
            emit(g, out)
        prev = rolled


def _block_kernel(x_ref, meta_ref, ng_ref, win_ref, caw_ref, cab_ref, lng_ref, lnb_ref,
                  wao_ref, bao_ref, cbw_ref, wbo_ref, wo_ref, fg_ref, o_ref,
                  h_ref, ua_buf, ca_buf, cx_buf, cvb_buf, pa_ref, pb_ref, m_ref,
                  mua_ref, mcx_ref):
    b = pl.program_id(0)
    j = pl.program_id(1)
    n_groups = TM // SUBLANES

    @pl.when((b == 0) & (j == 0))
    def _meta():
        hm = (_rms_scale(meta_ref[...]) * ng_ref[...]).astype(jnp.bfloat16)
        av = _dot(hm, win_ref[:, O_AV:O_AV + D])
        ag = _dot(hm, win_ref[:, O_AG:O_AG + D])
        mua_ref[0:HALO_A - N_META, :] = jnp.zeros((HALO_A - N_META, D), jnp.float32)
        mua_ref[HALO_A - N_META:HALO_A, :] = av * _sigmoid(ag)
        bc = _dot(hm, win_ref[:, O_BC:O_BC + D])
        bx = _dot(hm, win_ref[:, O_BX:O_BX + D])
        mcx_ref[...] = (bc * bx)[N_META - HALO_B:N_META, :]

    @pl.when(j == 0)
    def _halo_from_meta():
        ua_buf[0:HALO_A, :] = mua_ref[...]
        cx_buf[0:HALO_B, :] = mcx_ref[...]

    @pl.when(j > 0)
    def _halo_from_prev_tile():
        ua_buf[0:HALO_A, :] = ua_buf[TM:TM + HALO_A, :]
        cx_buf[0:HALO_B, :] = cx_buf[TM:TM + HALO_B, :]

    h_ref[...] = (_rms_scale(x_ref[0]) * ng_ref[...]).astype(jnp.bfloat16)

    for cb in range(D // CB):
        cols = slice(cb * CB, (cb + 1) * CB)
        av = _dot(h_ref[...], win_ref[:, O_AV + cb * CB:O_AV + (cb + 1) * CB])
        ag = _dot(h_ref[...], win_ref[:, O_AG + cb * CB:O_AG + (cb + 1) * CB])
        ua_buf[HALO_A:HALO_A + TM, cols] = av * _sigmoid(ag)
        for lb in range(CB // LANES):
            lanes = slice(cb * CB + lb * LANES, cb * CB + (lb + 1) * LANES)

            def emit_a(g, out, lanes=lanes):
                ca_buf[g * SUBLANES:(g + 1) * SUBLANES, lanes] = out + cab_ref[:, lanes]

            _causal_conv(ua_buf, caw_ref, CONV_A, HALO_A // SUBLANES, lanes, n_groups, emit_a)

    for g in range(n_groups):
        rows = slice(g * SUBLANES, (g + 1) * SUBLANES)
        c = ca_buf[rows, :]
        dlt = c - jnp.mean(c, axis=-1, keepdims=True)
        var = jnp.mean(dlt * dlt, axis=-1, keepdims=True)
        y = dlt * lax.rsqrt(var + EPS) * lng_ref[...] + lnb_ref[...]
        ca_buf[rows, :] = _silu(y)

    for cb in range(D // CB):
        cols = slice(cb * CB, (cb + 1) * CB)
        az = _dot(h_ref[...], win_ref[:, O_AZ + cb * CB:O_AZ + (cb + 1) * CB])
        pa_ref[:, cols] = (ca_buf[:, cols] * _silu(az)).astype(jnp.bfloat16)

    for cb in range(D // CB):
        cols = slice(cb * CB, (cb + 1) * CB)
        bc = _dot(h_ref[...], win_ref[:, O_BC + cb * CB:O_BC + (cb + 1) * CB])
        bx = _dot(h_ref[...], win_ref[:, O_BX + cb * CB:O_BX + (cb + 1) * CB])
        cx_buf[HALO_B:HALO_B + TM, cols] = bc * bx
        for lb in range(CB // LANES):
            lanes = slice(cb * CB + lb * LANES, cb * CB + (lb + 1) * LANES)
            loc = slice(lb * LANES, (lb + 1) * LANES)

            def emit_b(g, out, loc=loc):
                cvb_buf[g * SUBLANES:(g + 1) * SUBLANES, loc] = out

            _causal_conv(cx_buf, cbw_ref, CONV_B, HALO_B // SUBLANES, lanes, n_groups, emit_b)
        bb = _dot(h_ref[...], win_ref[:, O_BB + cb * CB:O_BB + (cb + 1) * CB])
        bz = _dot(h_ref[...], win_ref[:, O_BZ + cb * CB:O_BZ + (cb + 1) * CB])
        pb_ref[:, cols] = ((bb * cvb_buf[...]) * _silu(bz)).astype(jnp.bfloat16)

    for cb in range(D // CB):
        cols = slice(cb * CB, (cb + 1) * CB)
        ya = _dot(pa_ref[...], wao_ref[:, cols]) + bao_ref[:, cols]
        ga = _dot(h_ref[...], win_ref[:, O_GA + cb * CB:O_GA + (cb + 1) * CB])
        yb = _dot(pb_ref[...], wbo_ref[:, cols])
        gb = _dot(h_ref[...], win_ref[:, O_GB + cb * CB:O_GB + (cb + 1) * CB])
        m_ref[:, cols] = (_sigmoid(ga) * ya + _sigmoid(gb) * yb).astype(jnp.bfloat16)

    for cb in range(D // CB):
        cols = slice(cb * CB, (cb + 1) * CB)
        o_ref[0, :, cols] = x_ref[0, :, cols] + _dot(m_ref[...], wo_ref[:, cols])
    for g in range(n_groups):
        rows = slice(g * SUBLANES, (g + 1) * SUBLANES)
        o_ref[0, rows, :] = _rms_scale(o_ref[0, rows, :]) * fg_ref[...]


def _resident(shape):
    return pl.BlockSpec(shape, lambda b, j: (0,) * len(shape), pipeline_mode=pl.Buffered(1))


def kernel(x, meta_tokens, norm_g, w_in, conv_a_w, conv_a_b, ln_a_g, ln_a_b, w_a_out, b_a_out, conv_b_w, w_b_out, w_out, final_g):
    bsz, seq, d = x.shape
    assert d == D and seq % TM == 0 and meta_tokens.shape == (N_META, D)
    assert w_in.shape == (1, D, 9 * D), "single-layer block only"
    f32, bf16 = jnp.float32, jnp.bfloat16
    row = lambda v: v.reshape(1, D).astype(f32)
    rep = lambda w: jnp.broadcast_to(w.astype(f32)[:, None, :], (w.shape[0], SUBLANES, D))

    args = (
        x,
        meta_tokens.astype(f32),
        row(norm_g[0]),
        w_in[0].astype(bf16),
        rep(conv_a_w[0]),
        jnp.broadcast_to(row(conv_a_b[0]), (SUBLANES, D)),
        row(ln_a_g[0]),
        row(ln_a_b[0]),
        w_a_out[0].astype(bf16),
        row(b_a_out[0]),
        rep(conv_b_w[0]),
        w_b_out[0].astype(bf16),
        w_out[0].astype(bf16),
        row(final_g),
    )
    in_specs = [pl.BlockSpec((1, TM, D), lambda b, j: (b, j, 0))]
    in_specs += [_resident(a.shape) for a in args[1:]]
    scratch = [
        pltpu.VMEM((TM, D), bf16),
        pltpu.VMEM((HALO_A + TM, D), f32),
        pltpu.VMEM((TM, D), f32),
        pltpu.VMEM((HALO_B + TM, D), f32),
        pltpu.VMEM((TM, CB), f32),
        pltpu.VMEM((TM, D), bf16),
        pltpu.VMEM((TM, D), bf16),
        pltpu.VMEM((TM, D), bf16),
        pltpu.VMEM((HALO_A, D), f32),
        pltpu.VMEM((HALO_B, D), f32),
    ]
    return pl.pallas_call(
        _block_kernel,
        grid=(bsz, seq // TM),
        in_specs=in_specs,
        out_specs=pl.BlockSpec((1, TM, D), lambda b, j: (b, j, 0)),
        out_shape=jax.ShapeDtypeStruct((bsz, seq, D), x.dtype),
        scratch_shapes=scratch,
        compiler_params=pltpu.CompilerParams(
            dimension_semantics=("arbitrary", "arbitrary"),
            vmem_limit_bytes=VMEM_LIMIT_BYTES,
        ),
        name="gated_conv_block",
    )(*args)
```

```python
import itertools

import jax
import jax.numpy as jnp
from jax import lax
from jax.experimental import pallas as pl
from jax.experimental.pallas import tpu as pltpu

D = 1024
N_META = 16
CONV_A = 31
CONV_B = 3
EPS = 1e-6

SUBLANES = 8
LANES = 128
TM = 256
CB = 256
NCB = D // CB
N_GROUPS = TM // SUBLANES
HALO_A = 32
HALO_B = 8
W_CHUNK = 512
VMEM_LIMIT_BYTES = 56 * 1024 * 1024
NEG_LOG2E = -1.4426950408889634

O_AV, O_AG, O_AZ, O_BB, O_BC, O_BX, O_BZ, O_GA, O_GB = (i * D for i in range(9))


def _dot(a, b):
    return jnp.dot(a, b, preferred_element_type=jnp.float32)


def _sigmoid(v):
    return 1.0 / (1.0 + jnp.exp2(v * NEG_LOG2E))


def _silu(v):
    return v * _sigmoid(v)


def _rms_scale(v):
    return v * lax.rsqrt(jnp.mean(v * v, axis=-1, keepdims=True) + EPS)


def _causal_conv_units(buf, wb_ref, ktaps, halo_groups, lanes, emit):
    nq = (ktaps - 1) // SUBLANES + 1
    nr = min(SUBLANES, ktaps)
    row = lax.broadcasted_iota(jnp.int32, (SUBLANES, LANES), 0)
    masks = [row >= r for r in range(nr)]
    prev = None
    for g in range(-1, N_GROUPS):
        us = []
        for q in range(nq):
            r0 = SUBLANES * (halo_groups + g - q)
            us.append(buf[r0:r0 + SUBLANES, lanes])
        rolled = []
        for r in range(nr):
            if g < 0 and r == 0:
                rolled.append(None)
                continue
            acc = None
            for q in range(nq):
                s = SUBLANES * q + r
                if s >= ktaps:
                    continue
                term = wb_ref[ktaps - 1 - s, :, lanes] * us[q]
                acc = term if acc is None else acc + term
            rolled.append(acc if r == 0 else pltpu.roll(acc, r, 0))
        if g >= 0:
            out = rolled[0]
            for r in range(1, nr):
                out = out + jnp.where(masks[r], rolled[r], prev[r])
            emit(g, out)
            yield
        prev = rolled


def _pull(units, n):
    for _ in range(n):
        next(units, None)


def _block_kernel(x_ref, meta_ref, ng_ref, win_hbm, caw_ref, cab_ref, lng_ref, lnb_ref,
                  wao_hbm, bao_ref, cbw_ref, wbo_hbm, wo_hbm, fg_ref, o_ref,
                  win_ref, wao_ref, wbo_ref, wo_ref, wstage, wsem,
                  h_ref, ua_buf, ca_buf, cx_buf, cvb_buf, saz_buf, sga_buf, sgb_buf,
                  pa_ref, pb_ref, m_ref, mua_ref, mcx_ref):
    b = pl.program_id(0)
    j = pl.program_id(1)

    @pl.when((b == 0) & (j == 0))
    def _load_weights():
        chunks = [(src, dst, c0)
                  for src, dst in ((win_hbm, win_ref), (wao_hbm, wao_ref), (wbo_hbm, wbo_ref), (wo_hbm, wo_ref))
                  for c0 in range(0, dst.shape[1], W_CHUNK)]

        def copy(i):
            src, _, c0 = chunks[i]
            slot = i % 2
            return pltpu.make_async_copy(src.at[0, :, pl.ds(c0, W_CHUNK)], wstage.at[slot], wsem.at[slot])

        copy(0).start()
        for i, (_, dst, c0) in enumerate(chunks):
            if i + 1 < len(chunks):
                copy(i + 1).start()
            copy(i).wait()
            dst[:, c0:c0 + W_CHUNK] = wstage[i % 2].astype(jnp.bfloat16)

    def cols_of(cb):
        return slice(cb * CB, (cb + 1) * CB)

    def proj(off, cb):
        return _dot(h_ref[...], win_ref[:, off + cb * CB:off + (cb + 1) * CB])

    @pl.when((b == 0) & (j == 0))
    def _meta():
        hm = (_rms_scale(meta_ref[...]) * ng_ref[...]).astype(jnp.bfloat16)
        av = _dot(hm, win_ref[:, O_AV:O_AV + D])
        ag = _dot(hm, win_ref[:, O_AG:O_AG + D])
        mua_ref[0:HALO_A - N_META, :] = jnp.zeros((HALO_A - N_META, D), jnp.float32)
        mua_ref[HALO_A - N_META:HALO_A, :] = av * _sigmoid(ag)
        bc = _dot(hm, win_ref[:, O_BC:O_BC + D])
        bx = _dot(hm, win_ref[:, O_BX:O_BX + D])
        mcx_ref[...] = (bc * bx)[N_META - HALO_B:N_META, :]

    @pl.when(j == 0)
    def _halo_from_meta():
        ua_buf[0:HALO_A, :] = mua_ref[...]
        cx_buf[0:HALO_B, :] = mcx_ref[...]

    @pl.when(j > 0)
    def _halo_from_prev_tile():
        ua_buf[0:HALO_A, :] = ua_buf[TM:TM + HALO_A, :]
        cx_buf[0:HALO_B, :] = cx_buf[TM:TM + HALO_B, :]

    h_ref[...] = (_rms_scale(x_ref[0]) * ng_ref[...]).astype(jnp.bfloat16)

    def glu_task(cb):
        def run():
            ua_buf[HALO_A:HALO_A + TM, cols_of(cb)] = proj(O_AV, cb) * _sigmoid(proj(O_AG, cb))
        return 2, run

    def cx_task(cb):
        def run():
            cx_buf[HALO_B:HALO_B + TM, cols_of(cb)] = proj(O_BC, cb) * proj(O_BX, cb)
        return 2, run

    def pb_task(cb):
        def run():
            cols = cols_of(cb)
            pb_ref[:, cols] = ((proj(O_BB, cb) * cvb_buf[:, cols]) * _silu(proj(O_BZ, cb))).astype(jnp.bfloat16)
        return 2, run

    def gate_task(buf, off, fn, cb):
        def run():
            buf[:, cols_of(cb)] = fn(proj(off, cb))
        return 1, run

    def conv31_units(cb):
        def one(lb):
            lanes = slice(cb * CB + lb * LANES, cb * CB + (lb + 1) * LANES)

            def emit(g, out):
                ca_buf[g * SUBLANES:(g + 1) * SUBLANES, lanes] = out + cab_ref[:, lanes]

            return _causal_conv_units(ua_buf, caw_ref, CONV_A, HALO_A // SUBLANES, lanes, emit)
        return itertools.chain(*[one(lb) for lb in range(CB // LANES)])

    def conv3_units(cb):
        def one(lb):
            lanes = slice(cb * CB + lb * LANES, cb * CB + (lb + 1) * LANES)

            def emit(g, out):
                cvb_buf[g * SUBLANES:(g + 1) * SUBLANES, lanes] = out

            return _causal_conv_units(cx_buf, cbw_ref, CONV_B, HALO_B // SUBLANES, lanes, emit)
        return itertools.chain(*[one(lb) for lb in range(CB // LANES)])

    def layernorm_units():
        for g in range(N_GROUPS):
            rows = slice(g * SUBLANES, (g + 1) * SUBLANES)
            c = ca_buf[rows, :]
            dlt = c - jnp.mean(c, axis=-1, keepdims=True)
            var = jnp.mean(dlt * dlt, axis=-1, keepdims=True)
            y = dlt * lax.rsqrt(var + EPS) * lng_ref[...] + lnb_ref[...]
            ca_buf[rows, :] = _silu(y)
            yield

    n_conv_units = (CB // LANES) * N_GROUPS

    glu_task(0)[1]()
    for cb in range(NCB):
        tasks = []
        if cb > 0:
            tasks.append(pb_task(cb - 1))
        i_cx = len(tasks)
        tasks.append(cx_task(cb))
        tasks.append(gate_task(saz_buf, O_AZ, _silu, cb))
        tasks.append(gate_task(sga_buf, O_GA, _sigmoid, cb))
        tasks.append(gate_task(sgb_buf, O_GB, _sigmoid, cb))
        if cb + 1 < NCB:
            tasks.append(glu_task(cb + 1))
        total = sum(w for w, _ in tasks)
        after_cx = sum(w for w, _ in tasks[i_cx + 1:])
        c31, c3 = conv31_units(cb), conv3_units(cb)
        for i, (w, run) in enumerate(tasks):
            run()
            _pull(c31, -(-n_conv_units * w // total))
            if i > i_cx:
                _pull(c3, -(-n_conv_units * w // after_cx))
        _pull(c31, n_conv_units)
        _pull(c3, n_conv_units)

    ln = layernorm_units()
    for w, run in [pb_task(NCB - 1)]:
        run()
        _pull(ln, N_GROUPS // 4)
    for cb in range(NCB):
        cols = cols_of(cb)
        sgb_buf[:, cols] = sgb_buf[:, cols] * _dot(pb_ref[...], wbo_ref[:, cols])
        _pull(ln, N_GROUPS // 4 if cb < NCB - 2 else N_GROUPS)
        if cb >= NCB - 2:
            for c2 in ((0, 1) if cb == NCB - 2 else (2, 3)):
                k = cols_of(c2)
                pa_ref[:, k] = (ca_buf[:, k] * saz_buf[:, k]).astype(jnp.bfloat16)

    for cb in range(NCB):
        cols = cols_of(cb)
        ya = _dot(pa_ref[...], wao_ref[:, cols]) + bao_ref[:, cols]
        m_ref[:, cols] = (sga_buf[:, cols] * ya + sgb_buf[:, cols]).astype(jnp.bfloat16)

    for cb in range(NCB):
        cols = cols_of(cb)
        o_ref[0, :, cols] = x_ref[0, :, cols] + _dot(m_ref[...], wo_ref[:, cols])
    for g in range(N_GROUPS):
        rows = slice(g * SUBLANES, (g + 1) * SUBLANES)
        o_ref[0, rows, :] = _rms_scale(o_ref[0, rows, :]) * fg_ref[...]


def _resident(shape):
    return pl.BlockSpec(shape, lambda b, j: (0,) * len(shape), pipeline_mode=pl.Buffered(1))


def kernel(x, meta_tokens, norm_g, w_in, conv_a_w, conv_a_b, ln_a_g, ln_a_b, w_a_out, b_a_out, conv_b_w, w_b_out, w_out, final_g):
    bsz, seq, d = x.shape
    assert d == D and seq % TM == 0 and meta_tokens.shape == (N_META, D)
    assert w_in.shape == (1, D, 9 * D), "single-layer block only"
    f32, bf16 = jnp.float32, jnp.bfloat16
    row = lambda v: v.reshape(1, D).astype(f32)
    rep = lambda w: jnp.broadcast_to(w.astype(f32)[:, None, :], (w.shape[0], SUBLANES, D))

    hbm = pl.BlockSpec(memory_space=pl.ANY)
    operands = (
        (x, pl.BlockSpec((1, TM, D), lambda b, j: (b, j, 0))),
        (meta_tokens.astype(f32), None),
        (row(norm_g[0]), None),
        (w_in.astype(f32), hbm),
        (rep(conv_a_w[0]), None),
        (jnp.broadcast_to(row(conv_a_b[0]), (SUBLANES, D)), None),
        (row(ln_a_g[0]), None),
        (row(ln_a_b[0]), None),
        (w_a_out.astype(f32), hbm),
        (row(b_a_out[0]), None),
        (rep(conv_b_w[0]), None),
        (w_b_out.astype(f32), hbm),
        (w_out.astype(f32), hbm),
        (row(final_g), None),
    )
    args = [a for a, _ in operands]
    in_specs = [_resident(a.shape) if s is None else s for a, s in operands]
    scratch = [
        pltpu.VMEM((D, 9 * D), bf16),
        pltpu.VMEM((D, D), bf16),
        pltpu.VMEM((D, D), bf16),
        pltpu.VMEM((D, D), bf16),
        pltpu.VMEM((2, D, W_CHUNK), f32),
        pltpu.SemaphoreType.DMA((2,)),
        pltpu.VMEM((TM, D), bf16),
        pltpu.VMEM((HALO_A + TM, D), f32),
        pltpu.VMEM((TM, D), f32),
        pltpu.VMEM((HALO_B + TM, D), f32),
        pltpu.VMEM((TM, D), f32),
        pltpu.VMEM((TM, D), f32),
        pltpu.VMEM((TM, D), f32),
        pltpu.VMEM((TM, D), f32),
        pltpu.VMEM((TM, D), bf16),
        pltpu.VMEM((TM, D), bf16),
        pltpu.VMEM((TM, D), bf16),
        pltpu.VMEM((HALO_A, D), f32),
        pltpu.VMEM((HALO_B, D), f32),
    ]
    return pl.pallas_call(
        _block_kernel,
        grid=(bsz, seq // TM),
        in_specs=in_specs,
        out_specs=pl.BlockSpec((1, TM, D), lambda b, j: (b, j, 0)),
        out_shape=jax.ShapeDtypeStruct((bsz, seq, D), x.dtype),
        scratch_shapes=scratch,
        compiler_params=pltpu.CompilerParams(
            dimension_semantics=("arbitrary", "arbitrary"),
            vmem_limit_bytes=VMEM_LIMIT_BYTES,
        ),
        name="gated_conv_block",
    )(*args)
```

```python
import itertools

import jax
import jax.numpy as jnp
from jax import lax
from jax.experimental import pallas as pl
from jax.experimental.pallas import tpu as pltpu

D = 1024
N_META = 16
CONV_A = 31
CONV_B = 3
EPS = 1e-6

SUBLANES = 8
LANES = 128
TM = 512
N_SUB = 1
CB = 256
NCB = D // CB
N_GROUPS = TM // SUBLANES
HALO_A = 32
HALO_B = 8
W_CHUNK = 128
VMEM_LIMIT_BYTES = 58 * 1024 * 1024
NEG_LOG2E = -1.4426950408889634

O_AV, O_AG, O_AZ, O_BB, O_BC, O_BX, O_BZ, O_GA, O_GB = (i * D for i in range(9))

STAGES_WEIGHT = 35
TAIL_WEIGHT = 15


def _dot(a, b):
    return jnp.dot(a, b, preferred_element_type=jnp.float32)


def _sigmoid(v):
    return 1.0 / (1.0 + jnp.exp2(v * NEG_LOG2E))


def _silu(v):
    return v * _sigmoid(v)


def _rms_scale(v):
    return v * lax.rsqrt(jnp.mean(v * v, axis=-1, keepdims=True) + EPS)


def _causal_conv_units(buf, wb_ref, ktaps, halo_groups, lanes, emit):
    nq = (ktaps - 1) // SUBLANES + 1
    nr = min(SUBLANES, ktaps)
    row = lax.broadcasted_iota(jnp.int32, (SUBLANES, LANES), 0)
    masks = [row >= r for r in range(nr)]
    prev = None
    for g in range(-1, N_GROUPS):
        us = []
        for q in range(nq):
            r0 = SUBLANES * (halo_groups + g - q)
            us.append(buf[r0:r0 + SUBLANES, lanes])
        rolled = []
        for r in range(nr):
            if g < 0 and r == 0:
                rolled.append(None)
                continue
            acc = None
            for q in range(nq):
                s = SUBLANES * q + r
                if s >= ktaps:
                    continue
                term = wb_ref[ktaps - 1 - s, :, lanes] * us[q]
                acc = term if acc is None else acc + term
            rolled.append(acc if r == 0 else pltpu.roll(acc, r, 0))
        if g >= 0:
            out = rolled[0]
            for r in range(1, nr):
                out = out + jnp.where(masks[r], rolled[r], prev[r])
            emit(g, out)
            yield
        prev = rolled


def _pull(units, n):
    for _ in range(n):
        next(units, None)


def _run(gen):
    for _ in gen:
        pass


def _merge(a, weight_a, b, weight_b):
    done = {0: 0, 1: 0}
    gens = {0: a, 1: b}
    total = {0: weight_a, 1: weight_b}
    while gens:
        k = min(gens, key=lambda i: done[i] / total[i])
        try:
            done[k] += next(gens[k])
        except StopIteration:
            del gens[k]


def _block_kernel(x_ref, meta_ref, ng_ref, win_hbm, caw_ref, cab_ref, lng_ref, lnb_ref,
                  wao_hbm, bao_ref, cbw_ref, wbo_hbm, wo_hbm, fg_ref, o_ref,
                  win_ref, wao_ref, wbo_ref, wo_ref, wstage, wsem,
                  h_all, ua_all, ca_all, cx_all, cvb_all, saz_all, sga_all, sgb_all, pb_all,
                  pa_ref, m_ref, mua_ref, mcx_ref):
    b = pl.program_id(0)
    j = pl.program_id(1)

    @pl.when((b == 0) & (j == 0))
    def _load_weights():
        chunks = [(src, dst, c0)
                  for src, dst in ((win_hbm, win_ref), (wao_hbm, wao_ref), (wbo_hbm, wbo_ref), (wo_hbm, wo_ref))
                  for c0 in range(0, dst.shape[1], W_CHUNK)]

        def copy(i):
            src, _, c0 = chunks[i]
            slot = i % 2
            return pltpu.make_async_copy(src.at[0, :, pl.ds(c0, W_CHUNK)], wstage.at[slot], wsem.at[slot])

        copy(0).start()
        for i, (_, dst, c0) in enumerate(chunks):
            if i + 1 < len(chunks):
                copy(i + 1).start()
            copy(i).wait()
            dst[:, c0:c0 + W_CHUNK] = wstage[i % 2].astype(jnp.bfloat16)

    @pl.when((b == 0) & (j == 0))
    def _meta():
        hm = (_rms_scale(meta_ref[...]) * ng_ref[...]).astype(jnp.bfloat16)
        av = _dot(hm, win_ref[:, O_AV:O_AV + D])
        ag = _dot(hm, win_ref[:, O_AG:O_AG + D])
        mua_ref[0:HALO_A - N_META, :] = jnp.zeros((HALO_A - N_META, D), jnp.float32)
        mua_ref[HALO_A - N_META:HALO_A, :] = av * _sigmoid(ag)
        bc = _dot(hm, win_ref[:, O_BC:O_BC + D])
        bx = _dot(hm, win_ref[:, O_BX:O_BX + D])
        mcx_ref[...] = (bc * bx)[N_META - HALO_B:N_META, :]

    @pl.when(j == 0)
    def _halo_from_meta():
        ua_all[0, 0:HALO_A, :] = mua_ref[...]
        cx_all[0, 0:HALO_B, :] = mcx_ref[...]

    @pl.when(j > 0)
    def _halo_from_prev_step():
        ua_all[0, 0:HALO_A, :] = ua_all[N_SUB - 1, TM:TM + HALO_A, :]
        cx_all[0, 0:HALO_B, :] = cx_all[N_SUB - 1, TM:TM + HALO_B, :]

    def cols_of(cb):
        return slice(cb * CB, (cb + 1) * CB)

    def lhs(ref):
        return ref[...]

    def sub_tile(s):
        xrows = slice(s * TM, (s + 1) * TM)
        h_ref, ua_buf, ca_buf, cx_buf, cvb_buf = h_all.at[s], ua_all.at[s], ca_all.at[s], cx_all.at[s], cvb_all.at[s]
        saz_buf, sga_buf, sgb_buf, pb_ref = saz_all.at[s], sga_all.at[s], sgb_all.at[s], pb_all.at[s]

        def proj(off, cb):
            return _dot(lhs(h_ref), win_ref[:, off + cb * CB:off + (cb + 1) * CB])

        def glu_task(cb):
            def run():
                ua_buf[HALO_A:HALO_A + TM, cols_of(cb)] = proj(O_AV, cb) * _sigmoid(proj(O_AG, cb))
            return 2, run

        def cx_task(cb):
            def run():
                cx_buf[HALO_B:HALO_B + TM, cols_of(cb)] = proj(O_BC, cb) * proj(O_BX, cb)
            return 2, run

        def pb_task(cb):
            def run():
                cols = cols_of(cb)
                pb_ref[:, cols] = ((proj(O_BB, cb) * cvb_buf[:, cols]) * _silu(proj(O_BZ, cb))).astype(jnp.bfloat16)
            return 2, run

        def gate_task(buf, off, fn, cb):
            def run():
                buf[:, cols_of(cb)] = fn(proj(off, cb))
            return 1, run

        def conv31_units(cb):
            def one(lb):
                lanes = slice(cb * CB + lb * LANES, cb * CB + (lb + 1) * LANES)

                def emit(g, out):
                    ca_buf[g * SUBLANES:(g + 1) * SUBLANES, lanes] = out + cab_ref[:, lanes]

                return _causal_conv_units(ua_buf, caw_ref, CONV_A, HALO_A // SUBLANES, lanes, emit)
            return itertools.chain(*[one(lb) for lb in range(CB // LANES)])

        def conv3_units(cb):
            def one(lb):
                lanes = slice(cb * CB + lb * LANES, cb * CB + (lb + 1) * LANES)

                def emit(g, out):
                    cvb_buf[g * SUBLANES:(g + 1) * SUBLANES, lanes] = out

                return _causal_conv_units(cx_buf, cbw_ref, CONV_B, HALO_B // SUBLANES, lanes, emit)
            return itertools.chain(*[one(lb) for lb in range(CB // LANES)])

        def layernorm_units():
            for g in range(N_GROUPS):
                rows = slice(g * SUBLANES, (g + 1) * SUBLANES)
                c = ca_buf[rows, :]
                dlt = c - jnp.mean(c, axis=-1, keepdims=True)
                var = jnp.mean(dlt * dlt, axis=-1, keepdims=True)
                y = dlt * lax.rsqrt(var + EPS) * lng_ref[...] + lnb_ref[...]
                ca_buf[rows, :] = _silu(y)
                yield

        n_conv_units = (CB // LANES) * N_GROUPS

        def stages():
            if s > 0:
                ua_buf[0:HALO_A, :] = ua_all[s - 1, TM:TM + HALO_A, :]
                cx_buf[0:HALO_B, :] = cx_all[s - 1, TM:TM + HALO_B, :]
            h_ref[...] = (_rms_scale(x_ref[0, xrows, :]) * ng_ref[...]).astype(jnp.bfloat16)
            yield 1
            w, run = glu_task(0)
            run()
            yield w
            for cb in range(NCB):
                tasks = []
                if cb > 0:
                    tasks.append(pb_task(cb - 1))
                i_cx = len(tasks)
                tasks.append(cx_task(cb))
                tasks.append(gate_task(saz_buf, O_AZ, _silu, cb))
                tasks.append(gate_task(sga_buf, O_GA, _sigmoid, cb))
                tasks.append(gate_task(sgb_buf, O_GB, _sigmoid, cb))
                if cb + 1 < NCB:
                    tasks.append(glu_task(cb + 1))
                total = sum(w for w, _ in tasks)
                after_cx = sum(w for w, _ in tasks[i_cx + 1:])
                c31, c3 = conv31_units(cb), conv3_units(cb)
                for i, (w, run) in enumerate(tasks):
                    run()
                    _pull(c31, -(-n_conv_units * w // total))
                    if i > i_cx:
                        _pull(c3, -(-n_conv_units * w // after_cx))
                    if i + 1 == len(tasks):
                        _pull(c31, n_conv_units)
                        _pull(c3, n_conv_units)
                    yield w

        def tail():
            ln = layernorm_units()
            w, run = pb_task(NCB - 1)
            run()
            _pull(ln, N_GROUPS // 4)
            yield w
            for cb in range(NCB):
                cols = cols_of(cb)
                sgb_buf[:, cols] = sgb_buf[:, cols] * _dot(lhs(pb_ref), wbo_ref[:, cols])
                _pull(ln, N_GROUPS // 4 if cb < NCB - 2 else N_GROUPS)
                if cb >= NCB - 2:
                    for c2 in ((0, 1) if cb == NCB - 2 else (2, 3)):
                        k = cols_of(c2)
                        pa_ref[:, k] = (ca_buf[:, k] * saz_buf[:, k]).astype(jnp.bfloat16)
                yield 1
            for cb in range(NCB):
                cols = cols_of(cb)
                ya = _dot(lhs(pa_ref), wao_ref[:, cols]) + bao_ref[:, cols]
                m_ref[:, cols] = (sga_buf[:, cols] * ya + sgb_buf[:, cols]).astype(jnp.bfloat16)
                yield 1
            for cb in range(NCB):
                cols = cols_of(cb)
                o_ref[0, xrows, cols] = x_ref[0, xrows, cols] + _dot(lhs(m_ref), wo_ref[:, cols])
                yield 1
            for g in range(N_GROUPS):
                rows = slice(s * TM + g * SUBLANES, s * TM + (g + 1) * SUBLANES)
                o_ref[0, rows, :] = _rms_scale(o_ref[0, rows, :]) * fg_ref[...]
            yield 1

        return stages, tail

    subs = [sub_tile(s) for s in range(N_SUB)]
    _run(subs[0][0]())
    for s in range(1, N_SUB):
        _merge(subs[s - 1][1](), TAIL_WEIGHT, subs[s][0](), STAGES_WEIGHT)
    _run(subs[N_SUB - 1][1]())


def _resident(shape):
    return pl.BlockSpec(shape, lambda b, j: (0,) * len(shape), pipeline_mode=pl.Buffered(1))


def kernel(x, meta_tokens, norm_g, w_in, conv_a_w, conv_a_b, ln_a_g, ln_a_b, w_a_out, b_a_out, conv_b_w, w_b_out, w_out, final_g):
    bsz, seq, d = x.shape
    rows_per_step = TM * N_SUB
    assert d == D and seq % rows_per_step == 0 and meta_tokens.shape == (N_META, D)
    assert w_in.shape == (1, D, 9 * D), "single-layer block only"
    f32, bf16 = jnp.float32, jnp.bfloat16
    row = lambda v: v.reshape(1, D).astype(f32)
    rep = lambda w: jnp.broadcast_to(w.astype(f32)[:, None, :], (w.shape[0], SUBLANES, D))

    hbm = pl.BlockSpec(memory_space=pl.ANY)
    tile = pl.BlockSpec((1, rows_per_step, D), lambda b, j: (b, j, 0))
    operands = (
        (x, tile),
        (meta_tokens.astype(f32), None),
        (row(norm_g[0]), None),
        (w_in.astype(f32), hbm),
        (rep(conv_a_w[0]), None),
        (jnp.broadcast_to(row(conv_a_b[0]), (SUBLANES, D)), None),
        (row(ln_a_g[0]), None),
        (row(ln_a_b[0]), None),
        (w_a_out.astype(f32), hbm),
        (row(b_a_out[0]), None),
        (rep(conv_b_w[0]), None),
        (w_b_out.astype(f32), hbm),
        (w_out.astype(f32), hbm),
        (row(final_g), None),
    )
    args = [a for a, _ in operands]
    in_specs = [_resident(a.shape) if s is None else s for a, s in operands]
    scratch = [
        pltpu.VMEM((D, 9 * D), bf16),
        pltpu.VMEM((D, D), bf16),
        pltpu.VMEM((D, D), bf16),
        pltpu.VMEM((D, D), bf16),
        pltpu.VMEM((2, D, W_CHUNK), f32),
        pltpu.SemaphoreType.DMA((2,)),
        pltpu.VMEM((N_SUB, TM, D), bf16),
        pltpu.VMEM((N_SUB, HALO_A + TM, D), f32),
        pltpu.VMEM((N_SUB, TM, D), f32),
        pltpu.VMEM((N_SUB, HALO_B + TM, D), f32),
        pltpu.VMEM((N_SUB, TM, D), f32),
        pltpu.VMEM((N_SUB, TM, D), f32),
        pltpu.VMEM((N_SUB, TM, D), f32),
        pltpu.VMEM((N_SUB, TM, D), f32),
        pltpu.VMEM((N_SUB, TM, D), bf16),
        pltpu.VMEM((TM, D), bf16),
        pltpu.VMEM((TM, D), bf16),
        pltpu.VMEM((HALO_A, D), f32),
        pltpu.VMEM((HALO_B, D), f32),
    ]
    return pl.pallas_call(
        _block_kernel,
        grid=(bsz, seq // rows_per_step),
        in_specs=in_specs,
        out_specs=tile,
        out_shape=jax.ShapeDtypeStruct((bsz, seq, D), x.dtype),
        scratch_shapes=scratch,
        compiler_params=pltpu.CompilerParams(
            dimension_semantics=("arbitrary", "arbitrary"),
            vmem_limit_bytes=VMEM_LIMIT_BYTES,
        ),
        name="gated_conv_block",
    )(*args)
```

```python
import itertools

import jax
import jax.numpy as jnp
from jax import lax
from jax.experimental import pallas as pl
from jax.experimental.pallas import tpu as pltpu

D = 1024
N_META = 16
CONV_A = 31
CONV_B = 3
EPS = 1e-6

SUBLANES = 8
LANES = 128
TM = 256
CB = 256
NCB = D // CB
N_GROUPS = TM // SUBLANES
HALO_A = 32
HALO_B = 8
W_CHUNK = 512
W_SLOTS = 4
VMEM_LIMIT_BYTES = 56 * 1024 * 1024
NEG_LOG2E = -1.4426950408889634

O_AV, O_AG, O_AZ, O_BB, O_BC, O_BX, O_BZ, O_GA, O_GB = (i * D for i in range(9))


def _dot(a, b):
    return jnp.dot(a, b, preferred_element_type=jnp.float32)


def _sigmoid(v):
    return 1.0 / (1.0 + jnp.exp2(v * NEG_LOG2E))


def _silu(v):
    return v * _sigmoid(v)


def _rms_scale(v):
    return v * lax.rsqrt(jnp.mean(v * v, axis=-1, keepdims=True) + EPS)


def _causal_conv_units(buf, wb_ref, ktaps, halo_groups, lanes, emit):
    nq = (ktaps - 1) // SUBLANES + 1
    nr = min(SUBLANES, ktaps)
    row = lax.broadcasted_iota(jnp.int32, (SUBLANES, LANES), 0)
    masks = [row >= r for r in range(nr)]
    prev = None
    for g in range(-1, N_GROUPS):
        us = []
        for q in range(nq):
            r0 = SUBLANES * (halo_groups + g - q)
            us.append(buf[r0:r0 + SUBLANES, lanes])
        rolled = []
        for r in range(nr):
            if g < 0 and r == 0:
                rolled.append(None)
                continue
            acc = None
            for q in range(nq):
                s = SUBLANES * q + r
                if s >= ktaps:
                    continue
                term = wb_ref[ktaps - 1 - s, :, lanes] * us[q]
                acc = term if acc is None else acc + term
            rolled.append(acc if r == 0 else pltpu.roll(acc, r, 0))
        if g >= 0:
            out = rolled[0]
            for r in range(1, nr):
                out = out + jnp.where(masks[r], rolled[r], prev[r])
            emit(g, out)
            yield
        prev = rolled


def _pull(units, n):
    for _ in range(n):
        next(units, None)


def _block_kernel(x_ref, meta_ref, ng_ref, win_hbm, caw_ref, cab_ref, lng_ref, lnb_ref,
                  wao_hbm, bao_ref, cbw_ref, wbo_hbm, wo_hbm, fg_ref, o_ref,
                  win_ref, wao_ref, wbo_ref, wo_ref, wstage, wsem,
                  h_ref, ua_buf, ca_buf, cx_buf, cvb_buf, az_buf, ga_buf, gb_buf,
                  pa_ref, pb_ref, m_ref, mua_ref, mcx_ref):
    b = pl.program_id(0)
    j = pl.program_id(1)

    @pl.when((b == 0) & (j == 0))
    def _load_weights():
        chunks = [(src, dst, c0)
                  for src, dst in ((win_hbm, win_ref), (wao_hbm, wao_ref), (wbo_hbm, wbo_ref), (wo_hbm, wo_ref))
                  for c0 in range(0, dst.shape[1], W_CHUNK)]

        def copy(i):
            src, _, c0 = chunks[i]
            slot = i % W_SLOTS
            return pltpu.make_async_copy(src.at[0, :, pl.ds(c0, W_CHUNK)], wstage.at[slot], wsem.at[slot])

        for i in range(W_SLOTS - 1):
            copy(i).start()
        for i, (_, dst, c0) in enumerate(chunks):
            if i + W_SLOTS - 1 < len(chunks):
                copy(i + W_SLOTS - 1).start()
            copy(i).wait()
            dst[:, c0:c0 + W_CHUNK] = wstage[i % W_SLOTS].astype(jnp.bfloat16)

    @pl.when((b == 0) & (j == 0))
    def _meta():
        hm = (_rms_scale(meta_ref[...]) * ng_ref[...]).astype(jnp.bfloat16)
        av = _dot(hm, win_ref[:, O_AV:O_AV + D])
        ag = _dot(hm, win_ref[:, O_AG:O_AG + D])
        mua_ref[0:HALO_A - N_META, :] = jnp.zeros((HALO_A - N_META, D), jnp.float32)
        mua_ref[HALO_A - N_META:HALO_A, :] = av * _sigmoid(ag)
        bc = _dot(hm, win_ref[:, O_BC:O_BC + D])
        bx = _dot(hm, win_ref[:, O_BX:O_BX + D])
        mcx_ref[...] = (bc * bx)[N_META - HALO_B:N_META, :]

    @pl.when(j == 0)
    def _halo_from_meta():
        ua_buf[0:HALO_A, :] = mua_ref[...]
        cx_buf[0:HALO_B, :] = mcx_ref[...]

    @pl.when(j > 0)
    def _halo_from_prev_tile():
        ua_buf[0:HALO_A, :] = ua_buf[TM:TM + HALO_A, :]
        cx_buf[0:HALO_B, :] = cx_buf[TM:TM + HALO_B, :]

    def cols_of(cb):
        return slice(cb * CB, (cb + 1) * CB)

    def proj(off, cb):
        return _dot(h_ref[...], win_ref[:, off + cb * CB:off + (cb + 1) * CB])

    def glu_task(cb):
        def run():
            ua_buf[HALO_A:HALO_A + TM, cols_of(cb)] = proj(O_AV, cb) * _sigmoid(proj(O_AG, cb))
        return 2, run

    def cx_task(cb):
        def run():
            cx_buf[HALO_B:HALO_B + TM, cols_of(cb)] = proj(O_BC, cb) * proj(O_BX, cb)
        return 2, run

    def pb_task(cb):
        def run():
            cols = cols_of(cb)
            pb_ref[:, cols] = ((proj(O_BB, cb) * cvb_buf[:, cols]) * _silu(proj(O_BZ, cb))).astype(jnp.bfloat16)
        return 2, run

    def gate_task(buf, off, fn, cb):
        def run():
            buf[:, cols_of(cb)] = fn(proj(off, cb))
        return 1, run

    def conv_units(buf, wb_ref, ktaps, halo, out_buf, cb):
        def one(lb):
            lanes = slice(cb * CB + lb * LANES, cb * CB + (lb + 1) * LANES)

            def emit(g, out):
                out_buf[g * SUBLANES:(g + 1) * SUBLANES, lanes] = out

            return _causal_conv_units(buf, wb_ref, ktaps, halo // SUBLANES, lanes, emit)
        return itertools.chain.from_iterable(zip(*[one(lb) for lb in range(CB // LANES)]))

    def layernorm_units():
        for g in range(N_GROUPS):
            rows = slice(g * SUBLANES, (g + 1) * SUBLANES)
            c = ca_buf[rows, :] + cab_ref[...]
            dlt = c - jnp.mean(c, axis=-1, keepdims=True)
            var = jnp.mean(dlt * dlt, axis=-1, keepdims=True)
            y = dlt * lax.rsqrt(var + EPS) * lng_ref[...] + lnb_ref[...]
            ca_buf[rows, :] = _silu(y)
            yield

    n_conv_units = (CB // LANES) * N_GROUPS

    h_ref[...] = (_rms_scale(x_ref[0]) * ng_ref[...]).astype(jnp.bfloat16)
    glu_task(0)[1]()
    for cb in range(NCB):
        tasks = []
        if cb > 0:
            tasks.append(pb_task(cb - 1))
        i_cx = len(tasks)
        tasks.append(cx_task(cb))
        tasks.append(gate_task(az_buf, O_AZ, _silu, cb))
        tasks.append(gate_task(ga_buf, O_GA, _sigmoid, cb))
        tasks.append(gate_task(gb_buf, O_GB, _sigmoid, cb))
        if cb + 1 < NCB:
            tasks.append(glu_task(cb + 1))
        total = sum(w for w, _ in tasks)
        after_cx = sum(w for w, _ in tasks[i_cx + 1:])
        c31 = conv_units(ua_buf, caw_ref, CONV_A, HALO_A, ca_buf, cb)
        c3 = conv_units(cx_buf, cbw_ref, CONV_B, HALO_B, cvb_buf, cb)
        for i, (w, run) in enumerate(tasks):
            _pull(c31, -(-n_conv_units * w // total))
            if i > i_cx:
                _pull(c3, -(-n_conv_units * w // after_cx))
            run()
        _pull(c31, n_conv_units)
        _pull(c3, n_conv_units)

    ln = layernorm_units()
    pb_task(NCB - 1)[1]()
    _pull(ln, N_GROUPS // 4)
    for cb in range(NCB):
        cols = cols_of(cb)
        gb_buf[:, cols] = gb_buf[:, cols] * _dot(pb_ref[...], wbo_ref[:, cols])
        _pull(ln, N_GROUPS // 4 if cb < NCB - 2 else N_GROUPS)
        if cb >= NCB - 2:
            for c2 in ((0, 1) if cb == NCB - 2 else (2, 3)):
                k = cols_of(c2)
                pa_ref[:, k] = (ca_buf[:, k] * az_buf[:, k]).astype(jnp.bfloat16)

    for cb in range(NCB):
        cols = cols_of(cb)
        ya = _dot(pa_ref[...], wao_ref[:, cols]) + bao_ref[:, cols]
        m_ref[:, cols] = (ga_buf[:, cols] * ya + gb_buf[:, cols]).astype(jnp.bfloat16)

    for cb in range(NCB):
        cols = cols_of(cb)
        o_ref[0, :, cols] = x_ref[0, :, cols] + _dot(m_ref[...], wo_ref[:, cols])
    for g in range(N_GROUPS):
        rows = slice(g * SUBLANES, (g + 1) * SUBLANES)
        o_ref[0, rows, :] = _rms_scale(o_ref[0, rows, :]) * fg_ref[...]


def _resident(shape):
    return pl.BlockSpec(shape, lambda b, j: (0,) * len(shape), pipeline_mode=pl.Buffered(1))


def kernel(x, meta_tokens, norm_g, w_in, conv_a_w, conv_a_b, ln_a_g, ln_a_b, w_a_out, b_a_out, conv_b_w, w_b_out, w_out, final_g):
    bsz, seq, d = x.shape
    assert d == D and seq % TM == 0 and meta_tokens.shape == (N_META, D)
    assert w_in.shape == (1, D, 9 * D), "single-layer block only"
    f32, bf16 = jnp.float32, jnp.bfloat16
    row = lambda v: v.reshape(1, D).astype(f32)
    rep = lambda w: jnp.broadcast_to(w.astype(f32)[:, None, :], (w.shape[0], SUBLANES, D))

    hbm = pl.BlockSpec(memory_space=pl.ANY)
    tile = pl.BlockSpec((1, TM, D), lambda b, j: (b, j, 0))
    operands = (
        (x, tile),
        (meta_tokens.astype(f32), None),
        (row(norm_g[0]), None),
        (w_in.astype(f32), hbm),
        (rep(conv_a_w[0]), None),
        (jnp.broadcast_to(row(conv_a_b[0]), (SUBLANES, D)), None),
        (row(ln_a_g[0]), None),
        (row(ln_a_b[0]), None),
        (w_a_out.astype(f32), hbm),
        (row(b_a_out[0]), None),
        (rep(conv_b_w[0]), None),
        (w_b_out.astype(f32), hbm),
        (w_out.astype(f32), hbm),
        (row(final_g), None),
    )
    args = [a for a, _ in operands]
    in_specs = [_resident(a.shape) if s is None else s for a, s in operands]
    scratch = [
        pltpu.VMEM((D, 9 * D), bf16),
        pltpu.VMEM((D, D), bf16),
        pltpu.VMEM((D, D), bf16),
        pltpu.VMEM((D, D), bf16),
        pltpu.VMEM((W_SLOTS, D, W_CHUNK), f32),
        pltpu.SemaphoreType.DMA((W_SLOTS,)),
        pltpu.VMEM((TM, D), bf16),
        pltpu.VMEM((HALO_A + TM, D), f32),
        pltpu.VMEM((TM, D), f32),
        pltpu.VMEM((HALO_B + TM, D), f32),
        pltpu.VMEM((TM, D), f32),
        pltpu.VMEM((TM, D), f32),
        pltpu.VMEM((TM, D), f32),
        pltpu.VMEM((TM, D), f32),
        pltpu.VMEM((TM, D), bf16),
        pltpu.VMEM((TM, D), bf16),
        pltpu.VMEM((TM, D), bf16),
        pltpu.VMEM((HALO_A, D), f32),
        pltpu.VMEM((HALO_B, D), f32),
    ]
    return pl.pallas_call(
        _block_kernel,
        grid=(bsz, seq // TM),
        in_specs=in_specs,
        out_specs=tile,
        out_shape=jax.ShapeDtypeStruct((bsz, seq, D), x.dtype),
        scratch_shapes=scratch,
        compiler_params=pltpu.CompilerParams(
            dimension_semantics=("arbitrary", "arbitrary"),
            vmem_limit_bytes=VMEM_LIMIT_BYTES,
        ),
        name="gated_conv_block",
    )(*args)
```

```python
import itertools

import jax
import jax.numpy as jnp
from jax import lax
from jax.experimental import pallas as pl
from jax.experimental.pallas import tpu as pltpu

D = 1024
N_META = 16
CONV_A = 31
CONV_B = 3
EPS = 1e-6

SUBLANES = 8
LANES = 128
TM = 256
CB = 256
NCB = D // CB
N_GROUPS = TM // SUBLANES
HALO_A = 32
HALO_B = 8
GROUPS_PER_UNIT = 2
W_CHUNK = 512
W_SLOTS = 4
VMEM_LIMIT_BYTES = 56 * 1024 * 1024
NEG_LOG2E = -1.4426950408889634

O_AV, O_AG, O_AZ, O_BB, O_BC, O_BX, O_BZ, O_GA, O_GB = (i * D for i in range(9))


def _dot(a, b):
    return jnp.dot(a, b, preferred_element_type=jnp.float32)


def _sigmoid(v):
    return 1.0 / (1.0 + jnp.exp2(v * NEG_LOG2E))


def _silu(v):
    return v * _sigmoid(v)


def _rms_scale(v):
    return v * lax.rsqrt(jnp.mean(v * v, axis=-1, keepdims=True) + EPS)


def _causal_conv_units(buf, wb_ref, ktaps, halo_groups, lanes, emit):
    nq = (ktaps - 1) // SUBLANES + 1
    nr = min(SUBLANES, ktaps)
    row = lax.broadcasted_iota(jnp.int32, (SUBLANES, LANES), 0)
    masks = [row >= r for r in range(nr)]
    prev = None
    groups = list(range(-1, N_GROUPS))
    for i0 in range(0, len(groups), GROUPS_PER_UNIT):
        batch = groups[i0:i0 + GROUPS_PER_UNIT]
        loaded = {}

        def data(gi):
            if gi not in loaded:
                r0 = SUBLANES * (halo_groups + gi)
                loaded[gi] = buf[r0:r0 + SUBLANES, lanes]
            return loaded[gi]

        acc = {g: [None] * nr for g in batch}
        for r in range(nr):
            for q in range(nq):
                s = SUBLANES * q + r
                if s >= ktaps:
                    continue
                wv = wb_ref[ktaps - 1 - s, :, lanes]
                for g in batch:
                    if g < 0 and r == 0:
                        continue
                    term = wv * data(g - q)
                    acc[g][r] = term if acc[g][r] is None else acc[g][r] + term
        for g in batch:
            rolled = [acc[g][0]] + [pltpu.roll(acc[g][r], r, 0) for r in range(1, nr)]
            if g >= 0:
                out = rolled[0]
                for r in range(1, nr):
                    out = out + jnp.where(masks[r], rolled[r], prev[r])
                emit(g, out)
                yield
            prev = rolled


def _pull(units, n):
    for _ in range(n):
        next(units, None)


def _block_kernel(x_ref, meta_ref, ng_ref, win_hbm, caw_ref, cab_ref, lng_ref, lnb_ref,
                  wao_hbm, bao_ref, cbw_ref, wbo_hbm, wo_hbm, fg_ref, o_ref,
                  win_ref, wao_ref, wbo_ref, wo_ref, wstage, wsem,
                  h_ref, ua_buf, ca_buf, cx_buf, cvb_buf, az_buf, ga_buf, gb_buf,
                  pa_ref, pb_ref, m_ref, mua_ref, mcx_ref):
    b = pl.program_id(0)
    j = pl.program_id(1)

    @pl.when((b == 0) & (j == 0))
    def _load_weights():
        chunks = [(src, dst, c0)
                  for src, dst in ((win_hbm, win_ref), (wao_hbm, wao_ref), (wbo_hbm, wbo_ref), (wo_hbm, wo_ref))
                  for c0 in range(0, dst.shape[1], W_CHUNK)]

        def copy(i):
            src, _, c0 = chunks[i]
            slot = i % W_SLOTS
            return pltpu.make_async_copy(src.at[0, :, pl.ds(c0, W_CHUNK)], wstage.at[slot], wsem.at[slot])

        for i in range(W_SLOTS - 1):
            copy(i).start()
        for i, (_, dst, c0) in enumerate(chunks):
            if i + W_SLOTS - 1 < len(chunks):
                copy(i + W_SLOTS - 1).start()
            copy(i).wait()
            dst[:, c0:c0 + W_CHUNK] = wstage[i % W_SLOTS].astype(jnp.bfloat16)

    @pl.when((b == 0) & (j == 0))
    def _meta():
        hm = (_rms_scale(meta_ref[...]) * ng_ref[...]).astype(jnp.bfloat16)
        av = _dot(hm, win_ref[:, O_AV:O_AV + D])
        ag = _dot(hm, win_ref[:, O_AG:O_AG + D])
        mua_ref[0:HALO_A - N_META, :] = jnp.zeros((HALO_A - N_META, D), jnp.float32)
        mua_ref[HALO_A - N_META:HALO_A, :] = av * _sigmoid(ag)
        bc = _dot(hm, win_ref[:, O_BC:O_BC + D])
        bx = _dot(hm, win_ref[:, O_BX:O_BX + D])
        mcx_ref[...] = (bc * bx)[N_META - HALO_B:N_META, :]

    @pl.when(j == 0)
    def _halo_from_meta():
        ua_buf[0:HALO_A, :] = mua_ref[...]
        cx_buf[0:HALO_B, :] = mcx_ref[...]

    @pl.when(j > 0)
    def _halo_from_prev_tile():
        ua_buf[0:HALO_A, :] = ua_buf[TM:TM + HALO_A, :]
        cx_buf[0:HALO_B, :] = cx_buf[TM:TM + HALO_B, :]

    def cols_of(cb):
        return slice(cb * CB, (cb + 1) * CB)

    def proj(off, cb):
        return _dot(h_ref[...], win_ref[:, off + cb * CB:off + (cb + 1) * CB])

    def glu_task(cb):
        def run():
            ua_buf[HALO_A:HALO_A + TM, cols_of(cb)] = proj(O_AV, cb) * _sigmoid(proj(O_AG, cb))
        return 2, run

    def cx_task(cb):
        def run():
            cx_buf[HALO_B:HALO_B + TM, cols_of(cb)] = proj(O_BC, cb) * proj(O_BX, cb)
        return 2, run

    def pb_task(cb):
        def run():
            cols = cols_of(cb)
            pb_ref[:, cols] = ((proj(O_BB, cb) * cvb_buf[:, cols]) * _silu(proj(O_BZ, cb))).astype(jnp.bfloat16)
        return 2, run

    def gate_task(buf, off, fn, cb):
        def run():
            buf[:, cols_of(cb)] = fn(proj(off, cb))
        return 1, run

    def conv_units(buf, wb_ref, ktaps, halo, out_buf, cb):
        def one(lb):
            lanes = slice(cb * CB + lb * LANES, cb * CB + (lb + 1) * LANES)

            def emit(g, out):
                out_buf[g * SUBLANES:(g + 1) * SUBLANES, lanes] = out

            return _causal_conv_units(buf, wb_ref, ktaps, halo // SUBLANES, lanes, emit)
        return itertools.chain.from_iterable(zip(*[one(lb) for lb in range(CB // LANES)]))

    def layernorm_units():
        for g in range(N_GROUPS):
            rows = slice(g * SUBLANES, (g + 1) * SUBLANES)
            c = ca_buf[rows, :] + cab_ref[...]
            dlt = c - jnp.mean(c, axis=-1, keepdims=True)
            var = jnp.mean(dlt * dlt, axis=-1, keepdims=True)
            y = dlt * lax.rsqrt(var + EPS) * lng_ref[...] + lnb_ref[...]
            ca_buf[rows, :] = _silu(y)
            yield

    n_conv_units = (CB // LANES) * N_GROUPS

    h_ref[...] = (_rms_scale(x_ref[0]) * ng_ref[...]).astype(jnp.bfloat16)
    glu_task(0)[1]()
    for cb in range(NCB):
        tasks = []
        if cb > 0:
            tasks.append(pb_task(cb - 1))
        i_cx = len(tasks)
        tasks.append(cx_task(cb))
        tasks.append(gate_task(az_buf, O_AZ, _silu, cb))
        tasks.append(gate_task(ga_buf, O_GA, _sigmoid, cb))
        tasks.append(gate_task(gb_buf, O_GB, _sigmoid, cb))
        if cb + 1 < NCB:
            tasks.append(glu_task(cb + 1))
        total = sum(w for w, _ in tasks)
        after_cx = sum(w for w, _ in tasks[i_cx + 1:])
        c31 = conv_units(ua_buf, caw_ref, CONV_A, HALO_A, ca_buf, cb)
        c3 = conv_units(cx_buf, cbw_ref, CONV_B, HALO_B, cvb_buf, cb)
        for i, (w, run) in enumerate(tasks):
            _pull(c31, -(-n_conv_units * w // total))
            if i > i_cx:
                _pull(c3, -(-n_conv_units * w // after_cx))
            run()
        _pull(c31, n_conv_units)
        _pull(c3, n_conv_units)

    ln = layernorm_units()
    pb_task(NCB - 1)[1]()
    _pull(ln, N_GROUPS // 4)
    for cb in range(NCB):
        cols = cols_of(cb)
        gb_buf[:, cols] = gb_buf[:, cols] * _dot(pb_ref[...], wbo_ref[:, cols])
        _pull(ln, N_GROUPS // 4 if cb < NCB - 2 else N_GROUPS)
        if cb >= NCB - 2:
            for c2 in ((0, 1) if cb == NCB - 2 else (2, 3)):
                k = cols_of(c2)
                pa_ref[:, k] = (ca_buf[:, k] * az_buf[:, k]).astype(jnp.bfloat16)

    for cb in range(NCB):
        cols = cols_of(cb)
        ya = _dot(pa_ref[...], wao_ref[:, cols]) + bao_ref[:, cols]
        m_ref[:, cols] = (ga_buf[:, cols] * ya + gb_buf[:, cols]).astype(jnp.bfloat16)

    for cb in range(NCB):
        cols = cols_of(cb)
        o_ref[0, :, cols] = x_ref[0, :, cols] + _dot(m_ref[...], wo_ref[:, cols])
    for g in range(N_GROUPS):
        rows = slice(g * SUBLANES, (g + 1) * SUBLANES)
        o_ref[0, rows, :] = _rms_scale(o_ref[0, rows, :]) * fg_ref[...]


def _resident(shape):
    return pl.BlockSpec(shape, lambda b, j: (0,) * len(shape), pipeline_mode=pl.Buffered(1))


def kernel(x, meta_tokens, norm_g, w_in, conv_a_w, conv_a_b, ln_a_g, ln_a_b, w_a_out, b_a_out, conv_b_w, w_b_out, w_out, final_g):
    bsz, seq, d = x.shape
    assert d == D and seq % TM == 0 and meta_tokens.shape == (N_META, D)
    assert w_in.shape == (1, D, 9 * D), "single-layer block only"
    f32, bf16 = jnp.float32, jnp.bfloat16
    row = lambda v: v.reshape(1, D).astype(f32)
    rep = lambda w: jnp.broadcast_to(w.astype(f32)[:, None, :], (w.shape[0], SUBLANES, D))

    hbm = pl.BlockSpec(memory_space=pl.ANY)
    tile = pl.BlockSpec((1, TM, D), lambda b, j: (b, j, 0))
    operands = (
        (x, tile),
        (meta_tokens.astype(f32), None),
        (row(norm_g[0]), None),
        (w_in.astype(f32), hbm),
        (rep(conv_a_w[0]), None),
        (jnp.broadcast_to(row(conv_a_b[0]), (SUBLANES, D)), None),
        (row(ln_a_g[0]), None),
        (row(ln_a_b[0]), None),
        (w_a_out.astype(f32), hbm),
        (row(b_a_out[0]), None),
        (rep(conv_b_w[0]), None),
        (w_b_out.astype(f32), hbm),
        (w_out.astype(f32), hbm),
        (row(final_g), None),
    )
    args = [a for a, _ in operands]
    in_specs = [_resident(a.shape) if s is None else s for a, s in operands]
    scratch = [
        pltpu.VMEM((D, 9 * D), bf16),
        pltpu.VMEM((D, D), bf16),
        pltpu.VMEM((D, D), bf16),
        pltpu.VMEM((D, D), bf16),
        pltpu.VMEM((W_SLOTS, D, W_CHUNK), f32),
        pltpu.SemaphoreType.DMA((W_SLOTS,)),
        pltpu.VMEM((TM, D), bf16),
        pltpu.VMEM((HALO_A + TM, D), f32),
        pltpu.VMEM((TM, D), f32),
        pltpu.VMEM((HALO_B + TM, D), f32),
        pltpu.VMEM((TM, D), f32),
        pltpu.VMEM((TM, D), f32),
        pltpu.VMEM((TM, D), f32),
        pltpu.VMEM((TM, D), f32),
        pltpu.VMEM((TM, D), bf16),
        pltpu.VMEM((TM, D), bf16),
        pltpu.VMEM((TM, D), bf16),
        pltpu.VMEM((HALO_A, D), f32),
        pltpu.VMEM((HALO_B, D), f32),
    ]
    return pl.pallas_call(
        _block_kernel,
        grid=(bsz, seq // TM),
        in_specs=in_specs,
        out_specs=tile,
        out_shape=jax.ShapeDtypeStruct((bsz, seq, D), x.dtype),
        scratch_shapes=scratch,
        compiler_params=pltpu.CompilerParams(
            dimension_semantics=("arbitrary", "arbitrary"),
            vmem_limit_bytes=VMEM_LIMIT_BYTES,
        ),
        name="gated_conv_block",
    )(*args)
```

```python
import itertools

import jax
import jax.numpy as jnp
from jax import lax
from jax.experimental import pallas as pl
from jax.experimental.pallas import tpu as pltpu

D = 1024
N_META = 16
CONV_A = 31
CONV_B = 3
EPS = 1e-6

SUBLANES = 8
LANES = 128
STAGGER = LANES
TM = 256
CB = 256
NCB = D // CB
N_GROUPS = TM // SUBLANES
HALO_A = 32
HALO_B = 8
GROUPS_PER_UNIT = 2
W_CHUNK = 512
W_SLOTS = 4
VMEM_LIMIT_BYTES = 56 * 1024 * 1024
NEG_LOG2E = -1.4426950408889634

O_AV, O_AG, O_AZ, O_BB, O_BC, O_BX, O_BZ, O_GA, O_GB = (i * D for i in range(9))


def _dot(a, b):
    return jnp.dot(a, b, preferred_element_type=jnp.float32)


def _sigmoid(v):
    return 1.0 / (1.0 + jnp.exp2(v * NEG_LOG2E))


def _silu(v):
    return v * _sigmoid(v)


def _rms_scale(v):
    return v * lax.rsqrt(jnp.mean(v * v, axis=-1, keepdims=True) + EPS)


def _causal_conv_units(buf, wb_ref, ktaps, halo_groups, lanes, emit):
    nq = (ktaps - 1) // SUBLANES + 1
    nr = min(SUBLANES, ktaps)
    row = lax.broadcasted_iota(jnp.int32, (SUBLANES, LANES), 0)
    masks = [row >= r for r in range(nr)]
    prev = None
    groups = list(range(-1, N_GROUPS))
    for i0 in range(0, len(groups), GROUPS_PER_UNIT):
        batch = groups[i0:i0 + GROUPS_PER_UNIT]
        loaded = {}

        def data(gi):
            if gi not in loaded:
                r0 = SUBLANES * (halo_groups + gi)
                loaded[gi] = buf[r0:r0 + SUBLANES, lanes]
            return loaded[gi]

        acc = {g: [None] * nr for g in batch}
        for r in range(nr):
            for q in range(nq):
                s = SUBLANES * q + r
                if s >= ktaps:
                    continue
                wv = wb_ref[ktaps - 1 - s, :, lanes]
                for g in batch:
                    if g < 0 and r == 0:
                        continue
                    term = wv * data(g - q)
                    acc[g][r] = term if acc[g][r] is None else acc[g][r] + term
        for g in batch:
            rolled = [acc[g][0]] + [pltpu.roll(acc[g][r], r, 0) for r in range(1, nr)]
            if g >= 0:
                out = rolled[0]
                for r in range(1, nr):
                    out = out + jnp.where(masks[r], rolled[r], prev[r])
                emit(g, out)
                yield
            prev = rolled


def _pull(units, n):
    for _ in range(n):
        next(units, None)


def _block_kernel(x_ref, meta_ref, ng_ref, win_hbm, caw_ref, cab_ref, lng_ref, lnb_ref,
                  wao_hbm, bao_ref, cbw_ref, wbo_hbm, wo_hbm, fg_ref, o_ref,
                  win_ref, wao_ref, wbo_ref, wo_ref, wstage, wsem,
                  h_ref, ua_buf, ca_buf, cx_buf, cvb_buf, az_buf, ga_buf, gb_buf,
                  pa_ref, pb_ref, m_ref, mua_ref, mcx_ref):
    b = pl.program_id(0)
    j = pl.program_id(1)

    @pl.when((b == 0) & (j == 0))
    def _load_weights():
        chunks = [(src, dst, c0)
                  for src, dst in ((win_hbm, win_ref), (wao_hbm, wao_ref), (wbo_hbm, wbo_ref), (wo_hbm, wo_ref))
                  for c0 in range(0, src.shape[2], W_CHUNK)]

        def copy(i):
            src, _, c0 = chunks[i]
            slot = i % W_SLOTS
            return pltpu.make_async_copy(src.at[0, :, pl.ds(c0, W_CHUNK)], wstage.at[slot], wsem.at[slot])

        for i in range(W_SLOTS - 1):
            copy(i).start()
        for i, (_, dst, c0) in enumerate(chunks):
            if i + W_SLOTS - 1 < len(chunks):
                copy(i + W_SLOTS - 1).start()
            copy(i).wait()
            dst[:, c0:c0 + W_CHUNK] = wstage[i % W_SLOTS].astype(jnp.bfloat16)

    @pl.when((b == 0) & (j == 0))
    def _meta():
        hm = (_rms_scale(meta_ref[...]) * ng_ref[...]).astype(jnp.bfloat16)
        av = _dot(hm, win_ref[:, O_AV:O_AV + D])
        ag = _dot(hm, win_ref[:, O_AG:O_AG + D])
        mua_ref[0:HALO_A - N_META, :] = jnp.zeros((HALO_A - N_META, D), jnp.float32)
        mua_ref[HALO_A - N_META:HALO_A, :] = av * _sigmoid(ag)
        bc = _dot(hm, win_ref[:, O_BC:O_BC + D])
        bx = _dot(hm, win_ref[:, O_BX:O_BX + D])
        mcx_ref[...] = (bc * bx)[N_META - HALO_B:N_META, :]

    @pl.when(j == 0)
    def _halo_from_meta():
        ua_buf[0:HALO_A, 0:D] = mua_ref[...]
        cx_buf[0:HALO_B, 0:D] = mcx_ref[...]

    @pl.when(j > 0)
    def _halo_from_prev_tile():
        ua_buf[0:HALO_A, 0:D] = ua_buf[TM:TM + HALO_A, 0:D]
        cx_buf[0:HALO_B, 0:D] = cx_buf[TM:TM + HALO_B, 0:D]

    def cols_of(cb):
        return slice(cb * CB, (cb + 1) * CB)

    def proj(off, cb):
        return _dot(h_ref[:, 0:D], win_ref[:, off + cb * CB:off + (cb + 1) * CB])

    def glu_task(cb):
        def run():
            ua_buf[HALO_A:HALO_A + TM, cols_of(cb)] = proj(O_AV, cb) * _sigmoid(proj(O_AG, cb))
        return 2, run

    def cx_task(cb):
        def run():
            cx_buf[HALO_B:HALO_B + TM, cols_of(cb)] = proj(O_BC, cb) * proj(O_BX, cb)
        return 2, run

    def pb_task(cb):
        def run():
            cols = cols_of(cb)
            pb_ref[:, cols] = ((proj(O_BB, cb) * cvb_buf[:, cols]) * _silu(proj(O_BZ, cb))).astype(jnp.bfloat16)
        return 2, run

    def gate_task(buf, off, fn, cb):
        def run():
            buf[:, cols_of(cb)] = fn(proj(off, cb))
        return 1, run

    def conv_units(buf, wb_ref, ktaps, halo, out_buf, cb):
        def one(lb):
            lanes = slice(cb * CB + lb * LANES, cb * CB + (lb + 1) * LANES)

            def emit(g, out):
                out_buf[g * SUBLANES:(g + 1) * SUBLANES, lanes] = out

            return _causal_conv_units(buf, wb_ref, ktaps, halo // SUBLANES, lanes, emit)
        return itertools.chain.from_iterable(zip(*[one(lb) for lb in range(CB // LANES)]))

    def layernorm_units():
        for g in range(N_GROUPS):
            rows = slice(g * SUBLANES, (g + 1) * SUBLANES)
            c = ca_buf[rows, 0:D] + cab_ref[...]
            dlt = c - jnp.mean(c, axis=-1, keepdims=True)
            var = jnp.mean(dlt * dlt, axis=-1, keepdims=True)
            y = dlt * lax.rsqrt(var + EPS) * lng_ref[...] + lnb_ref[...]
            ca_buf[rows, 0:D] = _silu(y)
            yield

    n_conv_units = (CB // LANES) * N_GROUPS

    h_ref[:, 0:D] = (_rms_scale(x_ref[0]) * ng_ref[...]).astype(jnp.bfloat16)
    glu_task(0)[1]()
    for cb in range(NCB):
        tasks = []
        if cb > 0:
            tasks.append(pb_task(cb - 1))
        i_cx = len(tasks)
        tasks.append(cx_task(cb))
        tasks.append(gate_task(az_buf, O_AZ, _silu, cb))
        tasks.append(gate_task(ga_buf, O_GA, _sigmoid, cb))
        tasks.append(gate_task(gb_buf, O_GB, _sigmoid, cb))
        if cb + 1 < NCB:
            tasks.append(glu_task(cb + 1))
        total = sum(w for w, _ in tasks)
        after_cx = sum(w for w, _ in tasks[i_cx + 1:])
        c31 = conv_units(ua_buf, caw_ref, CONV_A, HALO_A, ca_buf, cb)
        c3 = conv_units(cx_buf, cbw_ref, CONV_B, HALO_B, cvb_buf, cb)
        for i, (w, run) in enumerate(tasks):
            _pull(c31, -(-n_conv_units * w // total))
            if i > i_cx:
                _pull(c3, -(-n_conv_units * w // after_cx))
            run()
        _pull(c31, n_conv_units)
        _pull(c3, n_conv_units)

    ln = layernorm_units()
    pb_task(NCB - 1)[1]()
    _pull(ln, N_GROUPS // 4)
    for cb in range(NCB):
        cols = cols_of(cb)
        gb_buf[:, cols] = gb_buf[:, cols] * _dot(pb_ref[:, 0:D], wbo_ref[:, cols])
        _pull(ln, N_GROUPS // 4 if cb < NCB - 2 else N_GROUPS)
        if cb >= NCB - 2:
            for c2 in ((0, 1) if cb == NCB - 2 else (2, 3)):
                k = cols_of(c2)
                pa_ref[:, k] = (ca_buf[:, k] * az_buf[:, k]).astype(jnp.bfloat16)

    for cb in range(NCB):
        cols = cols_of(cb)
        ya = _dot(pa_ref[:, 0:D], wao_ref[:, cols]) + bao_ref[:, cols]
        m_ref[:, cols] = (ga_buf[:, cols] * ya + gb_buf[:, cols]).astype(jnp.bfloat16)

    for cb in range(NCB):
        cols = cols_of(cb)
        o_ref[0, :, cols] = x_ref[0, :, cols] + _dot(m_ref[:, 0:D], wo_ref[:, cols])
    for g in range(N_GROUPS):
        rows = slice(g * SUBLANES, (g + 1) * SUBLANES)
        o_ref[0, rows, :] = _rms_scale(o_ref[0, rows, :]) * fg_ref[...]


def _resident(shape):
    return pl.BlockSpec(shape, lambda b, j: (0,) * len(shape), pipeline_mode=pl.Buffered(1))


def kernel(x, meta_tokens, norm_g, w_in, conv_a_w, conv_a_b, ln_a_g, ln_a_b, w_a_out, b_a_out, conv_b_w, w_b_out, w_out, final_g):
    bsz, seq, d = x.shape
    assert d == D and seq % TM == 0 and meta_tokens.shape == (N_META, D)
    assert w_in.shape == (1, D, 9 * D), "single-layer block only"
    f32, bf16 = jnp.float32, jnp.bfloat16
    row = lambda v: v.reshape(1, D).astype(f32)
    rep = lambda w: jnp.pad(jnp.broadcast_to(w.astype(f32)[:, None, :], (w.shape[0], SUBLANES, D)),
                            ((0, 0), (0, 0), (0, STAGGER)))

    hbm = pl.BlockSpec(memory_space=pl.ANY)
    tile = pl.BlockSpec((1, TM, D), lambda b, j: (b, j, 0))
    operands = (
        (x, tile),
        (meta_tokens.astype(f32), None),
        (row(norm_g[0]), None),
        (w_in.astype(f32), hbm),
        (rep(conv_a_w[0]), None),
        (jnp.broadcast_to(row(conv_a_b[0]), (SUBLANES, D)), None),
        (row(ln_a_g[0]), None),
        (row(ln_a_b[0]), None),
        (w_a_out.astype(f32), hbm),
        (row(b_a_out[0]), None),
        (rep(conv_b_w[0]), None),
        (w_b_out.astype(f32), hbm),
        (w_out.astype(f32), hbm),
        (row(final_g), None),
    )
    args = [a for a, _ in operands]
    in_specs = [_resident(a.shape) if s is None else s for a, s in operands]
    scratch = [
        pltpu.VMEM((D, 9 * D + STAGGER), bf16),
        pltpu.VMEM((D, D + STAGGER), bf16),
        pltpu.VMEM((D, D + STAGGER), bf16),
        pltpu.VMEM((D, D + STAGGER), bf16),
        pltpu.VMEM((W_SLOTS, D, W_CHUNK), f32),
        pltpu.SemaphoreType.DMA((W_SLOTS,)),
        pltpu.VMEM((TM, D), bf16),
        pltpu.VMEM((HALO_A + TM, D), f32),
        pltpu.VMEM((TM, D), f32),
        pltpu.VMEM((HALO_B + TM, D), f32),
        pltpu.VMEM((TM, D), f32),
        pltpu.VMEM((TM, D), f32),
        pltpu.VMEM((TM, D), f32),
        pltpu.VMEM((TM, D), f32),
        pltpu.VMEM((TM, D), bf16),
        pltpu.VMEM((TM, D), bf16),
        pltpu.VMEM((TM, D), bf16),
        pltpu.VMEM((HALO_A, D), f32),
        pltpu.VMEM((HALO_B, D), f32),
    ]
    return pl.pallas_call(
        _block_kernel,
        grid=(bsz, seq // TM),
        in_specs=in_specs,
        out_specs=tile,
        out_shape=jax.ShapeDtypeStruct((bsz, seq, D), x.dtype),
        scratch_shapes=scratch,
        compiler_params=pltpu.CompilerParams(
            dimension_semantics=("arbitrary", "arbitrary"),
            vmem_limit_bytes=VMEM_LIMIT_BYTES,
        ),
        name="gated_conv_block",
    )(*args)
```
